```python
import jax
import jax.numpy as jnp
from jax import lax
import numpy as np

D_MODEL = 2048
BATCH = 1
SEQ = 8192
DEPTH = 2
DEC_BATCH = 32
DEC_SEQ = 4
PAST_LEN = 8192
PAGE_SIZE = 128

N_HEADS = 8
HEAD_DIM = 128
N_KV_HEADS = 2
GROUP = N_HEADS // N_KV_HEADS
D_ATTN = N_HEADS * HEAD_DIM
D_KV = N_KV_HEADS * HEAD_DIM
N_KV_SLOTS = 4
CMP_BLOCK = 32
CMP_STRIDE = 16
CMP_RATIO = CMP_BLOCK // CMP_STRIDE
CMP_HIDDEN = 256
SLC_BLOCK = 64
N_SELECT = 16
WINDOW = 512
Q_BLOCK = 128
D_CONV = D_MODEL - D_ATTN
CONV_WIDTH = 31
D_FF = 5632
N_EXPERTS = 8
TOP_K = 2
D_FF_EXPERT = 5632
D_IN = D_ATTN + 6 * D_KV + 3 * N_HEADS + 2 * D_CONV
EPS = 1e-6
MASK_VALUE = -1e30
FORCE_SCORE = 1e4

kernel_name = 'nsa_conformer_hymba_decode_step'


def rms_norm(x, g):
    xf = x.astype(jnp.float32)
    y = xf * lax.rsqrt(jnp.mean(xf * xf, axis=-1, keepdims=True) + EPS)
    return (y * g.astype(jnp.float32)).astype(x.dtype)


def layer_norm(x, g, b):
    xf = x.astype(jnp.float32)
    mu = jnp.mean(xf, axis=-1, keepdims=True)
    var = jnp.mean(jnp.square(xf - mu), axis=-1, keepdims=True)
    y = (xf - mu) * lax.rsqrt(var + EPS)
    return (y * g.astype(jnp.float32) + b.astype(jnp.float32)).astype(x.dtype)


def adaln(c, w, b):
    m = jax.nn.silu(c) @ w + b
    return jnp.split(m[:, None, :], 6, axis=-1)


def modulate(x, g, shift, scale):
    return rms_norm(x, g) * (1.0 + scale) + shift


def masked_softmax(s, mask):
    s = jnp.where(mask, s.astype(jnp.float32), MASK_VALUE)
    e = jnp.where(mask, jnp.exp(s - jnp.max(s, axis=-1, keepdims=True)), 0.0)
    return e / jnp.maximum(jnp.sum(e, axis=-1, keepdims=True), 1e-30)


def project(h, w_in_l):
    B, T = h.shape[0], h.shape[1]
    sizes = (D_ATTN,) + (D_KV,) * 6 + (3 * N_HEADS, 2 * D_CONV)
    cuts = [int(v) for v in np.cumsum(sizes)[:-1]]
    q, kc, vc, ks, vs, kw, vw, g, u = jnp.split(h @ w_in_l, cuts, axis=-1)

    def heads(t):
        return t.reshape(B, T, N_KV_HEADS, GROUP, -1).transpose(0, 2, 3, 1, 4)

    def rows(t):
        return t.reshape(B, T, N_KV_HEADS, HEAD_DIM)

    kv = jnp.stack([rows(kc), rows(vc), rows(ks), rows(vs)], axis=1)
    win = jnp.stack([rows(kw), rows(vw)], axis=1)
    gates = jax.nn.sigmoid(heads(g))
    u_a, u_b = jnp.split(u, 2, axis=-1)
    return heads(q), gates, kv, win, u_a * jax.nn.sigmoid(u_b)


def compress_rows(rows, pe, w1, b1, w2, b2):
    B, L = rows.shape[0], rows.shape[1]
    n_chunk = -(-L // CMP_STRIDE)
    rows = jnp.pad(rows, ((0, 0), (0, n_chunk * CMP_STRIDE - L), (0, 0), (0, 0)))
    chunks = rows.reshape(B, n_chunk, CMP_STRIDE, N_KV_HEADS, HEAD_DIM)
    w1r = w1.reshape(CMP_RATIO, CMP_STRIDE, HEAD_DIM, CMP_HIDDEN)
    proj = jnp.einsum('bcshd,rsdf->brchf', chunks, w1r)
    nc = n_chunk - CMP_RATIO + 1
    hid = proj[:, 0, :nc]
    for r in range(1, CMP_RATIO):
        hid = hid + proj[:, r, r:r + nc]
    hid = jax.nn.gelu(hid + pe.reshape(-1) @ w1 + b1)
    out = hid @ w2 + b2
    return out.transpose(0, 2, 1, 3)


def compress_kv(k_rows, v_rows, pe, w1, b1, w2, b2):
    k = compress_rows(k_rows, pe[0], w1[0], b1[0], w2[0], b2[0])
    v = compress_rows(v_rows, pe[1], w1[1], b1[1], w2[1], b2[1])
    return k, v


def cmp_to_slc_matrix(nc, ns):
    ci = jnp.arange(nc)[:, None] * CMP_STRIDE
    sj = jnp.arange(ns)[None, :] * SLC_BLOCK
    ov = jnp.minimum(ci + CMP_BLOCK, sj + SLC_BLOCK) - jnp.maximum(ci, sj)
    return jnp.clip(ov, 0, None).astype(jnp.float32) / CMP_BLOCK


def nsa_attend(q, q_pos, gates, k_cmp, v_cmp, k_slc, v_slc, k_win, v_win, win_pos):
    scale = HEAD_DIM ** -0.5
    B = q.shape[0]
    L = k_slc.shape[2]
    nc = k_cmp.shape[2]
    s = jnp.einsum('bhgtd,bhnd->bhgtn', q, k_cmp) * scale
    cmp_end = jnp.arange(nc) * CMP_STRIDE + CMP_BLOCK - 1
    p_cmp = masked_softmax(s, cmp_end[None, :] <= q_pos[:, None])
    o_cmp = jnp.einsum('bhgtn,bhnd->bhgtd', p_cmp.astype(v_cmp.dtype), v_cmp)
    ns = -(-L // SLC_BLOCK)
    imp = jnp.einsum('bhgtn,nj->bhtj', p_cmp, cmp_to_slc_matrix(nc, ns))
    blk = jnp.arange(ns)[None, :]
    cur = q_pos[:, None] // SLC_BLOCK
    imp = jnp.where((blk == 0) | (blk == cur) | (blk == cur - 1), FORCE_SCORE, imp)
    imp = jnp.where(blk * SLC_BLOCK <= q_pos[:, None], imp, -FORCE_SCORE)
    n_sel = min(N_SELECT, ns)
    _, sel = lax.top_k(imp, n_sel)
    tok = (sel[..., None] * SLC_BLOCK + jnp.arange(SLC_BLOCK)).reshape(
        sel.shape[0], sel.shape[1], sel.shape[2], n_sel * SLC_BLOCK)
    tok_c = jnp.minimum(tok, L - 1)
    bi = jnp.arange(B)[:, None, None, None]
    hi = jnp.arange(N_KV_HEADS)[None, :, None, None]
    ks = k_slc[bi, hi, tok_c]
    vs = v_slc[bi, hi, tok_c]
    s = jnp.einsum('bhgtd,bhtsd->bhgts', q, ks) * scale
    p = masked_softmax(s, (tok <= q_pos[:, None])[:, :, None])
    o_slc = jnp.einsum('bhgts,bhtsd->bhgtd', p.astype(vs.dtype), vs)
    s = jnp.einsum('bhgtd,bhsd->bhgts', q, k_win) * scale
    dt = q_pos[:, None] - win_pos[None, :]
    p = masked_softmax(s, (dt >= 0) & (dt < WINDOW) & (win_pos[None, :] >= 0))
    o_win = jnp.einsum('bhgts,bhsd->bhgtd', p.astype(v_win.dtype), v_win)
    return gates[..., 0:1] * o_cmp + gates[..., 1:2] * o_slc + gates[..., 2:3] * o_win


def nsa_prompt(q, gates, k_cmp, v_cmp, k_slc, v_slc, k_win, v_win):
    B, T = q.shape[0], q.shape[3]
    nb = T // Q_BLOCK
    span = WINDOW + Q_BLOCK
    pad = ((0, 0), (0, 0), (WINDOW, 0), (0, 0))
    kw, vw = jnp.pad(k_win, pad), jnp.pad(v_win, pad)
    qb = jnp.moveaxis(q.reshape(B, N_KV_HEADS, GROUP, nb, Q_BLOCK, HEAD_DIM), 3, 0)
    gb = jnp.moveaxis(gates.reshape(B, N_KV_HEADS, GROUP, nb, Q_BLOCK, 3), 3, 0)

    def body(args):
        n, q_n, g_n = args
        p0 = n * Q_BLOCK
        k_n = lax.dynamic_slice_in_dim(kw, p0, span, axis=2)
        v_n = lax.dynamic_slice_in_dim(vw, p0, span, axis=2)
        return nsa_attend(q_n, p0 + jnp.arange(Q_BLOCK), g_n, k_cmp, v_cmp, k_slc, v_slc,
                          k_n, v_n, p0 - WINDOW + jnp.arange(span))

    out = lax.map(body, (jnp.arange(nb), qb, gb))
    return jnp.moveaxis(out, 0, 3).reshape(B, N_KV_HEADS, GROUP, T, HEAD_DIM)


def heads_major(r):
    return r.transpose(0, 2, 1, 3)


def conv_module(glu, prev, w, b, ln_g, ln_b):
    xcat = jnp.concatenate([prev.astype(glu.dtype), glu], axis=1)
    y = lax.conv_general_dilated(xcat, w[:, None, :].astype(glu.dtype), (1,), 'VALID',
                                 dimension_numbers=('NWC', 'WIO', 'NWC'),
                                 feature_group_count=D_CONV)
    y = jax.nn.silu(layer_norm(y + b, ln_g, ln_b))
    return y, xcat[:, -(CONV_WIDTH - 1):]


def merge_out(attn, conv, w_out_l):
    B, T = conv.shape[0], conv.shape[1]
    a = attn.transpose(0, 3, 1, 2, 4).reshape(B, T, D_ATTN)
    return jnp.concatenate([a, conv], axis=-1) @ w_out_l


def swiglu(h, w1, w3, w2):
    return (jax.nn.silu(h @ w1) * (h @ w3)) @ w2


def channel_mixer(h, l, ffn_w1, ffn_w3, ffn_w2, moe_router_w, moe_router_b, moe_w1, moe_w3, moe_w2):
    i = l // 2
    if l % 2 == 0:
        return swiglu(h, ffn_w1[i], ffn_w3[i], ffn_w2[i])
    logits = (h @ moe_router_w[i] + moe_router_b[i]).astype(jnp.float32)
    top_v, top_i = lax.top_k(logits, TOP_K)
    top_p = jax.nn.softmax(top_v, axis=-1)
    comb = jnp.sum(jax.nn.one_hot(top_i, N_EXPERTS, dtype=jnp.float32) * top_p[..., None],
                   axis=-2).astype(h.dtype)
    out = comb[..., 0:1] * swiglu(h, moe_w1[i, 0], moe_w3[i, 0], moe_w2[i, 0])
    for e in range(1, N_EXPERTS):
        out = out + comb[..., e:e + 1] * swiglu(h, moe_w1[i, e], moe_w3[i, e], moe_w2[i, e])
    return out


def setup_inputs(seed: int = 0) -> dict:
    key = jax.random.key(seed)
    k = jax.random.split(key, 40)
    n_pages = PAST_LEN // PAGE_SIZE
    n_used = DEC_BATCH * n_pages
    n_pool = n_used + (n_used + 3) // 4
    w_buf = min(WINDOW, PAST_LEN)
    n_dense = (DEPTH + 1) // 2
    n_moe = DEPTH // 2

    def nrm(kk, shape, scale):
        return jax.random.normal(kk, shape, jnp.float32) * scale

    perm = jax.random.permutation(k[5], n_pool)
    page_table = perm[:n_used].reshape(DEC_BATCH, n_pages).astype(jnp.int32)
    return {
        'x_prompt': nrm(k[0], (BATCH, SEQ, D_MODEL), 1.0),
        'x_sample': nrm(k[1], (DEC_BATCH, DEC_SEQ, D_MODEL), 1.0),
        'cache_kv': nrm(k[2], (n_pool, DEPTH, N_KV_SLOTS, PAGE_SIZE, N_KV_HEADS, HEAD_DIM), 1.0),
        'cache_win': nrm(k[3], (DEPTH, DEC_BATCH, 2, w_buf, N_KV_HEADS, HEAD_DIM), 1.0),
        'cache_conv': nrm(k[4], (DEPTH, DEC_BATCH, CONV_WIDTH - 1, D_CONV), 0.5),
        'page_table': page_table,
        'c_prompt': nrm(k[6], (BATCH, D_MODEL), 1.0),
        'c_sample': nrm(k[7], (DEC_BATCH, D_MODEL), 1.0),
        'w_ada': nrm(k[8], (DEPTH, D_MODEL, 6 * D_MODEL), D_MODEL ** -0.5),
        'b_ada': nrm(k[9], (DEPTH, 6 * D_MODEL), 0.02),
        'norm1_g': 1.0 + nrm(k[10], (DEPTH, D_MODEL), 0.02),
        'norm2_g': 1.0 + nrm(k[11], (DEPTH, D_MODEL), 0.02),
        'w_in': nrm(k[12], (DEPTH, D_MODEL, D_IN), D_MODEL ** -0.5),
        'w_out': nrm(k[13], (DEPTH, D_MODEL, D_MODEL), D_MODEL ** -0.5),
        'cmp_pe': nrm(k[14], (DEPTH, 2, CMP_BLOCK, HEAD_DIM), 0.02),
        'cmp_w1': nrm(k[15], (DEPTH, 2, CMP_BLOCK * HEAD_DIM, CMP_HIDDEN), (CMP_BLOCK * HEAD_DIM) ** -0.5),
        'cmp_b1': nrm(k[16], (DEPTH, 2, CMP_HIDDEN), 0.02),
        'cmp_w2': nrm(k[17], (DEPTH, 2, CMP_HIDDEN, HEAD_DIM), CMP_HIDDEN ** -0.5),
        'cmp_b2': nrm(k[18], (DEPTH, 2, HEAD_DIM), 0.02),
        'conv_w': nrm(k[19], (DEPTH, CONV_WIDTH, D_CONV), CONV_WIDTH ** -0.5),
        'conv_b': nrm(k[20], (DEPTH, D_CONV), 0.02),
        'conv_norm_g': 1.0 + nrm(k[21], (DEPTH, D_CONV), 0.02),
        'conv_norm_b': nrm(k[22], (DEPTH, D_CONV), 0.02),
        'ffn_w1': nrm(k[23], (n_dense, D_MODEL, D_FF), D_MODEL ** -0.5),
        'ffn_w3': nrm(k[24], (n_dense, D_MODEL, D_FF), D_MODEL ** -0.5),
        'ffn_w2': nrm(k[25], (n_dense, D_FF, D_MODEL), D_FF ** -0.5),
        'moe_router_w': nrm(k[26], (n_moe, D_MODEL, N_EXPERTS), D_MODEL ** -0.5),
        'moe_router_b': nrm(k[27], (n_moe, N_EXPERTS), 0.01),
        'moe_w1': nrm(k[28], (n_moe, N_EXPERTS, D_MODEL, D_FF_EXPERT), D_MODEL ** -0.5),
        'moe_w3': nrm(k[29], (n_moe, N_EXPERTS, D_MODEL, D_FF_EXPERT), D_MODEL ** -0.5),
        'moe_w2': nrm(k[30], (n_moe, N_EXPERTS, D_FF_EXPERT, D_MODEL), D_FF_EXPERT ** -0.5),
        'final_norm_g': 1.0 + nrm(k[31], (D_MODEL,), 0.02),
    }


def reference(x_prompt, x_sample, cache_kv, cache_win, cache_conv, page_table, c_prompt, c_sample,
              w_ada, b_ada, norm1_g, norm2_g, w_in, w_out,
              cmp_pe, cmp_w1, cmp_b1, cmp_w2, cmp_b2,
              conv_w, conv_b, conv_norm_g, conv_norm_b,
              ffn_w1, ffn_w3, ffn_w2,
              moe_router_w, moe_router_b, moe_w1, moe_w3, moe_w2,
              final_norm_g):
    n_pages = page_table.shape[1]
    w_buf = cache_win.shape[3]
    w_prompt = min(WINDOW, x_prompt.shape[1])
    pos_s = PAST_LEN + jnp.arange(DEC_SEQ)
    win_pos_s = PAST_LEN - w_buf + jnp.arange(w_buf + DEC_SEQ)
    xp, xs = x_prompt, x_sample
    kv_p, win_p, conv_p, kv_s, win_s, conv_s = [], [], [], [], [], []
    for l in range(DEPTH):
        cmp_args = (cmp_pe[l], cmp_w1[l], cmp_b1[l], cmp_w2[l], cmp_b2[l])
        conv_args = (conv_w[l], conv_b[l], conv_norm_g[l], conv_norm_b[l])
        ffn_args = (ffn_w1, ffn_w3, ffn_w2, moe_router_w, moe_router_b, moe_w1, moe_w3, moe_w2)

        sh1, sc1, gt1, sh2, sc2, gt2 = adaln(c_prompt, w_ada[l], b_ada[l])
        h = modulate(xp, norm1_g[l], sh1, sc1)
        q, gates, kv, win, glu = project(h, w_in[l])
        k_cmp, v_cmp = compress_kv(kv[:, 0], kv[:, 1], *cmp_args)
        attn = nsa_prompt(q, gates, k_cmp, v_cmp, heads_major(kv[:, 2]), heads_major(kv[:, 3]),
                          heads_major(win[:, 0]), heads_major(win[:, 1]))
        zeros_prev = jnp.zeros((glu.shape[0], CONV_WIDTH - 1, D_CONV), glu.dtype)
        conv, conv_state = conv_module(glu, zeros_prev, *conv_args)
        xp = xp + gt1 * merge_out(attn, conv, w_out[l])
        h = modulate(xp, norm2_g[l], sh2, sc2)
        xp = xp + gt2 * channel_mixer(h, l, *ffn_args)
        kv_p.append(kv)
        win_p.append(win[:, :, -w_prompt:])
        conv_p.append(conv_state)

        sh1, sc1, gt1, sh2, sc2, gt2 = adaln(c_sample, w_ada[l], b_ada[l])
        h = modulate(xs, norm1_g[l], sh1, sc1)
        q, gates, kv, win, glu = project(h, w_in[l])
        past = cache_kv[page_table, l]
        past = past.transpose(0, 2, 1, 3, 4, 5).reshape(
            DEC_BATCH, N_KV_SLOTS, n_pages * PAGE_SIZE, N_KV_HEADS, HEAD_DIM)
        full = jnp.concatenate([past, kv], axis=2)
        k_cmp, v_cmp = compress_kv(full[:, 0], full[:, 1], *cmp_args)
        wfull = jnp.concatenate([cache_win[l], win], axis=2)
        attn = nsa_attend(q, pos_s, gates, k_cmp, v_cmp, heads_major(full[:, 2]), heads_major(full[:, 3]),
                          heads_major(wfull[:, 0]), heads_major(wfull[:, 1]), win_pos_s)
        conv, conv_state = conv_module(glu, cache_conv[l], *conv_args)
        xs = xs + gt1 * merge_out(attn, conv, w_out[l])
        h = modulate(xs, norm2_g[l], sh2, sc2)
        xs = xs + gt2 * channel_mixer(h, l, *ffn_args)
        kv_s.append(kv)
        win_s.append(wfull[:, :, -w_buf:])
        conv_s.append(conv_state)

    y_prompt = rms_norm(xp, final_norm_g)
    y_sample = rms_norm(xs, final_norm_g)
    new_kv_prompt = jnp.stack(kv_p, axis=1)
    new_kv_sample = jnp.stack(kv_s, axis=1)
    new_win_prompt = jnp.stack(win_p, axis=0)
    new_win_sample = jnp.stack(win_s, axis=0)
    new_conv_prompt = jnp.stack(conv_p, axis=0)
    new_conv_sample = jnp.stack(conv_s, axis=0)
    return (y_prompt, y_sample, new_kv_prompt, new_kv_sample, new_win_prompt, new_win_sample, new_conv_prompt, new_conv_sample)
```

```python
import functools

import jax
import jax.numpy as jnp
import numpy as np
from jax import lax
from jax.experimental import pallas as pl
from jax.experimental.pallas import tpu as pltpu

D_MODEL = 2048
SEQ = 8192
DEPTH = 2
DEC_BATCH = 32
DEC_SEQ = 4
PAST_LEN = 8192
PAGE_SIZE = 128
N_HEADS = 8
HEAD_DIM = 128
N_KV_HEADS = 2
GROUP = N_HEADS // N_KV_HEADS
D_ATTN = N_HEADS * HEAD_DIM
D_KV = N_KV_HEADS * HEAD_DIM
CMP_BLOCK = 32
CMP_STRIDE = 16
CMP_HIDDEN = 256
SLC_BLOCK = 64
N_SELECT = 16
WINDOW = 512
Q_BLOCK = 128
D_CONV = D_MODEL - D_ATTN
CONV_WIDTH = 31
D_FF = 5632
N_EXPERTS = 8
EPS = 1e-6
MASK_VALUE = -1e30
FORCE_SCORE = 1e4
NEG_BIG = -3e38

LANES = 128
SUBLANES = 8

TM = 512
N_SAMPLE = DEC_BATCH * DEC_SEQ
A_ROWS = SEQ + TM
N_TILES = A_ROWS // TM
N_VALID = SEQ + N_SAMPLE
N_PAGES = PAST_LEN // PAGE_SIZE
PAGE_ROWS = PAGE_SIZE * N_KV_HEADS
CHUNKS_PER_PAGE = PAGE_SIZE // CMP_STRIDE
N_CMP = PAST_LEN // CMP_STRIDE
N_SLC_P = SEQ // SLC_BLOCK
N_SLC_S = -(-(PAST_LEN + DEC_SEQ) // SLC_BLOCK)
SCALE = HEAD_DIM ** -0.5

C_Q = 0
C_UA = 1024
C_UB = 2048
C_KV = 3072
C_WIN = 4096
N_PROJ = 4608
N_GATE = 3 * N_HEADS

f32 = jnp.float32
bf16 = jnp.bfloat16
i32 = jnp.int32


def _cparams(sem, vmem_mb=48):
    return pltpu.CompilerParams(dimension_semantics=sem, vmem_limit_bytes=vmem_mb * 1024 * 1024)


def _sigmoid(x):
    return 1.0 / (1.0 + jnp.exp(-x))


def _sel_tile(m):
    return (m == N_TILES - 1).astype(i32)


def _adaln_body(c_ref, w_ref, b_ref, o_ref):
    c = c_ref[...]
    a = (c * _sigmoid(c)).astype(bf16)
    o_ref[...] = jnp.dot(a, w_ref[...].astype(bf16), preferred_element_type=f32) + b_ref[...]


def _adaln(c_all, w, b):
    rows = c_all.shape[0]
    n = w.shape[1]
    tn = 1024
    return pl.pallas_call(
        _adaln_body,
        out_shape=jax.ShapeDtypeStruct((rows, n), f32),
        grid=(n // tn,),
        in_specs=[
            pl.BlockSpec((rows, D_MODEL), lambda j: (0, 0)),
            pl.BlockSpec((D_MODEL, tn), lambda j: (0, j)),
            pl.BlockSpec((1, tn), lambda j: (0, j)),
        ],
        out_specs=pl.BlockSpec((rows, tn), lambda j: (0, j)),
        compiler_params=_cparams(("arbitrary",)),
        name="adaln",
    )(c_all, w, b)


def _rms(x, g):
    ms = jnp.mean(x * x, axis=-1, keepdims=True)
    return x * lax.rsqrt(ms + EPS) * g


def _modulate_body(x_ref, g_ref, sh_ref, sc_ref, o_ref):
    y = _rms(x_ref[...], g_ref[...])
    o_ref[...] = (y * (1.0 + sc_ref[0]) + sh_ref[0]).astype(o_ref.dtype)


def _modulate_router_body(x_ref, g_ref, sh_ref, sc_ref, rw_ref, rb_ref, o_ref, hf_ref, ti_ref, tp_ref):
    y = _rms(x_ref[...], g_ref[...])
    h = y * (1.0 + sc_ref[0]) + sh_ref[0]
    hb = h.astype(bf16)
    o_ref[...] = hb
    hf_ref[...] = h
    lg = jnp.dot(hb, rw_ref[...].astype(bf16), preferred_element_type=f32) + rb_ref[...]
    lane = lax.broadcasted_iota(i32, lg.shape, 1)
    lg = jnp.where(lane < N_EXPERTS, lg, -jnp.inf)
    m1 = jnp.max(lg, axis=-1, keepdims=True)
    i1 = jnp.min(jnp.where(lg == m1, lane, LANES), axis=-1, keepdims=True)
    l2 = jnp.where(lane == i1, -jnp.inf, lg)
    m2 = jnp.max(l2, axis=-1, keepdims=True)
    i2 = jnp.min(jnp.where(l2 == m2, lane, LANES), axis=-1, keepdims=True)
    e = jnp.exp(m2 - m1)
    den = 1.0 + e
    ti_ref[...] = jnp.where(lane == 0, i1, jnp.where(lane == 1, i2, 0))
    tp_ref[...] = jnp.where(lane == 0, 1.0 / den, jnp.where(lane == 1, e / den, 0.0))


def _modulate(x, g, mod, k_shift, k_scale, router=None):
    in_specs = [
        pl.BlockSpec((TM, D_MODEL), lambda m: (m, 0)),
        pl.BlockSpec((1, D_MODEL), lambda m: (0, 0)),
        pl.BlockSpec((1, TM, D_MODEL), lambda m: (_sel_tile(m), 0, k_shift)),
        pl.BlockSpec((1, TM, D_MODEL), lambda m: (_sel_tile(m), 0, k_scale)),
    ]
    row_spec = pl.BlockSpec((TM, D_MODEL), lambda m: (m, 0))
    if router is None:
        return pl.pallas_call(
            _modulate_body,
            out_shape=jax.ShapeDtypeStruct((A_ROWS, D_MODEL), bf16),
            grid=(N_TILES,),
            in_specs=in_specs,
            out_specs=row_spec,
            compiler_params=_cparams(("arbitrary",)),
            name="modulate",
        )(x, g, mod, mod)
    rw, rb = router
    lane_spec = pl.BlockSpec((TM, LANES), lambda m: (m, 0))
    return pl.pallas_call(
        _modulate_router_body,
        out_shape=(
            jax.ShapeDtypeStruct((A_ROWS, D_MODEL), bf16),
            jax.ShapeDtypeStruct((A_ROWS, D_MODEL), f32),
            jax.ShapeDtypeStruct((A_ROWS, LANES), i32),
            jax.ShapeDtypeStruct((A_ROWS, LANES), f32),
        ),
        grid=(N_TILES,),
        in_specs=in_specs + [
            pl.BlockSpec((D_MODEL, LANES), lambda m: (0, 0)),
            pl.BlockSpec((1, LANES), lambda m: (0, 0)),
        ],
        out_specs=(row_spec, row_spec, lane_spec, lane_spec),
        compiler_params=_cparams(("arbitrary",)),
        name="modulate_router",
    )(x, g, mod, mod, rw, rb)


def _final_norm_body(x_ref, g_ref, o_ref):
    o_ref[...] = _rms(x_ref[...], g_ref[...])


def _final_norm(x, g):
    return pl.pallas_call(
        _final_norm_body,
        out_shape=jax.ShapeDtypeStruct((A_ROWS, D_MODEL), f32),
        grid=(N_TILES,),
        in_specs=[pl.BlockSpec((TM, D_MODEL), lambda m: (m, 0)), pl.BlockSpec((1, D_MODEL), lambda m: (0, 0))],
        out_specs=pl.BlockSpec((TM, D_MODEL), lambda m: (m, 0)),
        compiler_params=_cparams(("arbitrary",)),
        name="final_norm",
    )(x, g)


def _new_weights(te_ref, m):
    return (m == 0) | (te_ref[m] != te_ref[jnp.maximum(m - 1, 0)])


def _mm_body(te_ref, nu_ref, *refs, k_splits, has_res, n_out):
    n_lhs = len(k_splits)
    xs = refs[:n_lhs]
    w_ref = refs[n_lhs]
    pos = n_lhs + 1
    if has_res:
        res_ref, gate_ref = refs[pos], refs[pos + 1]
        pos += 2
    outs = refs[pos:pos + n_out]
    wb = refs[pos + n_out]
    m = pl.program_id(1)

    @pl.when(_new_weights(te_ref, m))
    def _():
        wb[...] = w_ref[0].astype(bf16)

    @pl.when(m < nu_ref[0])
    def _():
        acc = None
        for x_ref, (k0, k1) in zip(xs, k_splits):
            d = jnp.dot(x_ref[...], wb[k0:k1, :], preferred_element_type=f32)
            acc = d if acc is None else acc + d
        if has_res:
            acc = res_ref[...] + gate_ref[0] * acc
        for o in outs:
            o[...] = acc.astype(o.dtype)


def _row_tile(m, nu):
    return jnp.minimum(m, nu[0] - 1)


def _matmul(xs, w, te, nu, *, tn, out_dtypes, res=None, gate=None, gate_col=0, vmem_mb=48, name="matmul"):
    rows = xs[0].shape[0]
    n_tiles = rows // TM
    _, k_total, n = w.shape
    k_splits, k0 = [], 0
    for x in xs:
        k_splits.append((k0, k0 + x.shape[1]))
        k0 += x.shape[1]
    assert k0 == k_total and n % tn == 0
    in_specs = [pl.BlockSpec((TM, x.shape[1]), lambda j, m, te, nu: (_row_tile(m, nu), 0)) for x in xs]
    in_specs.append(pl.BlockSpec((1, k_total, tn), lambda j, m, te, nu: (te[_row_tile(m, nu)], 0, j)))
    args = list(xs) + [w]
    if res is not None:
        gc = gate_col * (D_MODEL // tn)
        in_specs.append(pl.BlockSpec((TM, tn), lambda j, m, te, nu: (_row_tile(m, nu), j)))
        in_specs.append(pl.BlockSpec((1, TM, tn), lambda j, m, te, nu: (_sel_tile(_row_tile(m, nu)), 0, gc + j)))
        args += [res, gate]
    out_spec = pl.BlockSpec((TM, tn), lambda j, m, te, nu: (_row_tile(m, nu), j))
    out_shape = tuple(jax.ShapeDtypeStruct((rows, n), dt) for dt in out_dtypes)
    body = functools.partial(_mm_body, k_splits=tuple(k_splits), has_res=res is not None, n_out=len(out_dtypes))
    out = pl.pallas_call(
        body,
        out_shape=out_shape,
        grid_spec=pltpu.PrefetchScalarGridSpec(
            num_scalar_prefetch=2,
            grid=(n // tn, n_tiles),
            in_specs=in_specs,
            out_specs=tuple(out_spec for _ in out_dtypes),
            scratch_shapes=[pltpu.VMEM((k_total, tn), bf16)],
        ),
        compiler_params=_cparams(("arbitrary", "arbitrary"), vmem_mb),
        name=name,
    )(te, nu, *args)
    return out


def _up_body(te_ref, nu_ref, x_ref, w1_ref, w3_ref, o_ref, w1b, w3b):
    m = pl.program_id(1)

    @pl.when(_new_weights(te_ref, m))
    def _():
        w1b[...] = w1_ref[0].astype(bf16)
        w3b[...] = w3_ref[0].astype(bf16)

    @pl.when(m < nu_ref[0])
    def _():
        x = x_ref[...]
        a = jnp.dot(x, w1b[...], preferred_element_type=f32)
        b = jnp.dot(x, w3b[...], preferred_element_type=f32)
        o_ref[...] = ((a * _sigmoid(a)) * b).astype(o_ref.dtype)


def _swiglu_up(x, w1, w3, te, nu, *, tf=512):
    rows = x.shape[0]
    n_tiles = rows // TM
    _, k, f = w1.shape
    w_spec = pl.BlockSpec((1, k, tf), lambda j, m, te, nu: (te[_row_tile(m, nu)], 0, j))
    return pl.pallas_call(
        _up_body,
        out_shape=jax.ShapeDtypeStruct((rows, f), bf16),
        grid_spec=pltpu.PrefetchScalarGridSpec(
            num_scalar_prefetch=2,
            grid=(f // tf, n_tiles),
            in_specs=[pl.BlockSpec((TM, k), lambda j, m, te, nu: (_row_tile(m, nu), 0)), w_spec, w_spec],
            out_specs=pl.BlockSpec((TM, tf), lambda j, m, te, nu: (_row_tile(m, nu), j)),
            scratch_shapes=[pltpu.VMEM((k, tf), bf16), pltpu.VMEM((k, tf), bf16)],
        ),
        compiler_params=_cparams(("arbitrary", "arbitrary")),
        name="swiglu_up",
    )(te, nu, x, w1, w3)


TC = 256


def _combine_body(y0_ref, y1_ref, x_ref, gate_ref, tp_ref, o_ref):
    tp = tp_ref[...]
    mix = tp[:, 0:1] * y0_ref[...] + tp[:, 1:2] * y1_ref[...]
    o_ref[...] = x_ref[...] + gate_ref[0] * mix


def _combine(y0, y1, x, mod, gate_col, top_p):
    per_tile = TM // TC
    row = pl.BlockSpec((TC, D_MODEL), lambda i: (i, 0))
    return pl.pallas_call(
        _combine_body,
        out_shape=jax.ShapeDtypeStruct((A_ROWS, D_MODEL), f32),
        grid=(A_ROWS // TC,),
        in_specs=[
            row, row, row,
            pl.BlockSpec((1, TC, D_MODEL), lambda i: (_sel_tile(i // per_tile), i % per_tile, gate_col)),
            pl.BlockSpec((TC, LANES), lambda i: (i, 0)),
        ],
        out_specs=row,
        compiler_params=_cparams(("arbitrary",)),
        name="moe_combine",
    )(y0, y1, x, mod, top_p)


PG = 8
N_CHUNK_PAD = N_CMP + SUBLANES


def _gelu_tanh(x):
    return x * (0.5 * (1.0 + jnp.tanh(np.sqrt(2.0 / np.pi).astype(np.float32) * (x + 0.044715 * (x * x * x)))))


def _compress_body(tab_ref, *refs):
    pages = refs[:PG]
    new_ref, w1_ref, b1_ref, pe_ref, w2_ref, b2_ref, o_ref, a_ref = refs[PG:]
    g = pl.program_id(1)
    half = CMP_STRIDE * HEAD_DIM

    @pl.when(g == 0)
    def _():
        row = lax.broadcasted_iota(i32, (SUBLANES, HEAD_DIM), 0)
        for h in range(N_KV_HEADS):
            for t in range(CMP_STRIDE):
                if t < DEC_SEQ:
                    r = N_KV_HEADS * t + h
                    v = jnp.where(row == 0, jnp.broadcast_to(new_ref[0, r:r + 1, :], (SUBLANES, HEAD_DIM)), 0.0)
                else:
                    v = jnp.zeros((SUBLANES, HEAD_DIM), f32)
                a_ref[h, N_CMP:N_CMP + SUBLANES, t * HEAD_DIM:(t + 1) * HEAD_DIM] = v

    for i in range(PG):
        r0 = pl.multiple_of((g * PG + i) * CHUNKS_PER_PAGE, CHUNKS_PER_PAGE)
        for h in range(N_KV_HEADS):
            for t in range(CMP_STRIDE):
                rows = pages[i][0, pl.ds(N_KV_HEADS * t + h, CHUNKS_PER_PAGE, stride=CMP_STRIDE * N_KV_HEADS), :]
                a_ref[h, pl.ds(r0, CHUNKS_PER_PAGE), t * HEAD_DIM:(t + 1) * HEAD_DIM] = rows

    @pl.when(g == N_PAGES // PG - 1)
    def _():
        w1 = w1_ref[0].astype(bf16)
        pe8 = jnp.broadcast_to(pe_ref[0], (SUBLANES, 2 * half)).astype(bf16)
        cvec = jnp.dot(pe8, w1, preferred_element_type=f32)[0:1, :] + b1_ref[0]
        w2 = w2_ref[0].astype(bf16)
        for h in range(N_KV_HEADS):
            ah = a_ref[h].astype(bf16)
            p0 = jnp.dot(ah, w1[:half], preferred_element_type=f32)
            p1 = jnp.dot(ah, w1[half:], preferred_element_type=f32)
            p1 = pltpu.roll(p1, N_CHUNK_PAD - 1, 0)
            hid = _gelu_tanh(p0[:N_CMP] + p1[:N_CMP] + cvec)
            o_ref[0, h] = jnp.dot(hid.astype(bf16), w2, preferred_element_type=f32) + b2_ref[0]


def _compress(pages, table, new_rows, w1, b1, pe, w2, b2):
    n_steps = table.shape[0]

    def page_spec(i):
        return pl.BlockSpec((1, PAGE_ROWS, HEAD_DIM), lambda s, g, tab: (tab[s, g * PG + i], 0, 0))

    kv = lambda s, g, tab: (s % 2, 0, 0)
    return pl.pallas_call(
        _compress_body,
        out_shape=jax.ShapeDtypeStruct((n_steps, N_KV_HEADS, N_CMP, HEAD_DIM), f32),
        grid_spec=pltpu.PrefetchScalarGridSpec(
            num_scalar_prefetch=1,
            grid=(n_steps, N_PAGES // PG),
            in_specs=[page_spec(i) for i in range(PG)] + [
                pl.BlockSpec((1, SUBLANES, HEAD_DIM), lambda s, g, tab: (s, 0, 0)),
                pl.BlockSpec((1, CMP_BLOCK * HEAD_DIM, CMP_HIDDEN), kv),
                pl.BlockSpec((1, 1, CMP_HIDDEN), kv),
                pl.BlockSpec((1, 1, CMP_BLOCK * HEAD_DIM), kv),
                pl.BlockSpec((1, CMP_HIDDEN, HEAD_DIM), kv),
                pl.BlockSpec((1, 1, HEAD_DIM), kv),
            ],
            out_specs=pl.BlockSpec((1, N_KV_HEADS, N_CMP, HEAD_DIM), lambda s, g, tab: (s, 0, 0, 0)),
            scratch_shapes=[pltpu.VMEM((N_KV_HEADS, N_CHUNK_PAD, CMP_STRIDE * HEAD_DIM), f32)],
        ),
        compiler_params=_cparams(("arbitrary", "arbitrary")),
        name="compress",
    )(table, *([pages] * PG), new_rows, w1, b1, pe, w2, b2)


def _col_softmax(s, valid):
    sm = jnp.where(valid, s, MASK_VALUE)
    mx = jnp.max(sm, axis=0, keepdims=True)
    e = jnp.where(valid, jnp.exp(sm - mx), 0.0)
    return e / jnp.maximum(jnp.sum(e, axis=0, keepdims=True), 1e-30)


def _select_blocks(imp, n_rows):
    j = lax.broadcasted_iota(i32, imp.shape, 0)
    sel = jnp.zeros(imp.shape, f32)
    for _ in range(N_SELECT):
        mx = jnp.max(imp, axis=0, keepdims=True)
        idx = jnp.min(jnp.where(imp == mx, j, n_rows), axis=0, keepdims=True)
        hit = j == idx
        sel = jnp.where(hit, 1.0, sel)
        imp = jnp.where(hit, NEG_BIG, imp)
    return sel


def _force_and_mask(imp, j, qpos):
    cur = qpos // SLC_BLOCK
    imp = jnp.where((j == 0) | (j == cur) | (j == cur - 1), FORCE_SCORE, imp)
    return jnp.where(j * SLC_BLOCK <= qpos, imp, -FORCE_SCORE)


KT = 512
W_SPAN = WINDOW + Q_BLOCK
NQ = GROUP * Q_BLOCK


def _nsa_prompt_body(q_ref, kc_ref, vc_ref, ks_ref, vst_ref, kw_ref, vwt_ref, mt_ref, gt_ref, o_ref,
                     qt_ref, acc_ref, m_ref, l_ref):
    n = pl.program_id(1)
    p0 = n * Q_BLOCK
    q = q_ref[...]
    qt = jnp.concatenate([q[:, g * HEAD_DIM:(g + 1) * HEAD_DIM].T for g in range(GROUP)], axis=1).astype(bf16)
    qt_ref[...] = qt
    qpos = p0 + lax.broadcasted_iota(i32, (1, NQ), 1) % Q_BLOCK

    sc = jnp.dot(kc_ref[0, 0].astype(bf16), qt, preferred_element_type=f32) * SCALE
    cmp_end = lax.broadcasted_iota(i32, (N_CMP, 1), 0) * CMP_STRIDE + (CMP_BLOCK - 1)
    pc = _col_softmax(sc, cmp_end <= qpos)
    oc = jnp.dot(vc_ref[0, 0].T.astype(bf16), pc.astype(bf16), preferred_element_type=f32)

    psum = pc[:, 0:Q_BLOCK]
    for g in range(1, GROUP):
        psum = psum + pc[:, g * Q_BLOCK:(g + 1) * Q_BLOCK]
    imp = jnp.dot(mt_ref[...], psum.astype(bf16), preferred_element_type=f32)
    jb = lax.broadcasted_iota(i32, (N_SLC_P, 1), 0)
    imp = _force_and_mask(imp, jb, qpos[:, 0:Q_BLOCK])
    sel = _select_blocks(imp, N_SLC_P).astype(bf16)

    m_ref[...] = jnp.full((1, NQ), MASK_VALUE, f32)
    l_ref[...] = jnp.zeros((1, NQ), f32)
    acc_ref[...] = jnp.zeros((HEAD_DIM, NQ), f32)
    row = lax.broadcasted_iota(i32, (KT, 1), 0)
    lane_blk = lax.broadcasted_iota(i32, (1, N_SLC_P), 1)

    def tile(kt, c):
        k0 = pl.multiple_of(kt * KT, KT)
        s = jnp.dot(ks_ref[pl.ds(k0, KT), :], qt_ref[...], preferred_element_type=f32) * SCALE
        expand = jnp.where(row // SLC_BLOCK + kt * (KT // SLC_BLOCK) == lane_blk, 1.0, 0.0).astype(bf16)
        mk = jnp.dot(expand, sel, preferred_element_type=f32)
        mk = jnp.concatenate([mk] * GROUP, axis=1)
        valid = (mk > 0.5) & (row + k0 <= qpos)
        sm = jnp.where(valid, s, MASK_VALUE)
        m_old = m_ref[...]
        m_new = jnp.maximum(m_old, jnp.max(sm, axis=0, keepdims=True))
        alpha = jnp.exp(m_old - m_new)
        e = jnp.where(valid, jnp.exp(sm - m_new), 0.0)
        l_ref[...] = alpha * l_ref[...] + jnp.sum(e, axis=0, keepdims=True)
        acc_ref[...] = alpha * acc_ref[...] + jnp.dot(vst_ref[0, kt], e.astype(bf16), preferred_element_type=f32)
        m_ref[...] = m_new
        return c

    lax.fori_loop(0, n // (KT // Q_BLOCK) + 1, tile, 0)
    os_ = acc_ref[...] / jnp.maximum(l_ref[...], 1e-30)

    start = pl.multiple_of(jnp.maximum(p0 - WINDOW, 0), Q_BLOCK)
    sw = jnp.dot(kw_ref[pl.ds(start, W_SPAN), :], qt, preferred_element_type=f32) * SCALE
    dt = qpos - (start + lax.broadcasted_iota(i32, (W_SPAN, 1), 0))
    pw = _col_softmax(sw, (dt >= 0) & (dt < WINDOW)).astype(bf16)
    t0 = start // Q_BLOCK
    ow = None
    for i in range(W_SPAN // Q_BLOCK):
        d = jnp.dot(vwt_ref[0, t0 + i], pw[i * Q_BLOCK:(i + 1) * Q_BLOCK], preferred_element_type=f32)
        ow = d if ow is None else ow + d

    gts = _sigmoid(gt_ref[0])
    outs = []
    for g in range(GROUP):
        cs = slice(g * Q_BLOCK, (g + 1) * Q_BLOCK)
        og = (gts[3 * g:3 * g + 1] * oc[:, cs] + gts[3 * g + 1:3 * g + 2] * os_[:, cs]
              + gts[3 * g + 2:3 * g + 3] * ow[:, cs])
        outs.append(og.T)
    o_ref[...] = jnp.concatenate(outs, axis=1).astype(o_ref.dtype)


def _nsa_prompt(p, pb, kcmp, vst, vwt, mt, gt):
    nb = SEQ // Q_BLOCK
    return pl.pallas_call(
        _nsa_prompt_body,
        out_shape=jax.ShapeDtypeStruct((SEQ, D_ATTN), bf16),
        grid=(N_KV_HEADS, nb),
        in_specs=[
            pl.BlockSpec((Q_BLOCK, NQ), lambda h, n: (n, h)),
            pl.BlockSpec((1, 1, N_CMP, HEAD_DIM), lambda h, n: (0, h, 0, 0)),
            pl.BlockSpec((1, 1, N_CMP, HEAD_DIM), lambda h, n: (1, h, 0, 0)),
            pl.BlockSpec((SEQ, HEAD_DIM), lambda h, n: (0, (C_KV + 2 * D_KV) // HEAD_DIM + h)),
            pl.BlockSpec((1, SEQ // KT, HEAD_DIM, KT), lambda h, n: (h, 0, 0, 0)),
            pl.BlockSpec((SEQ, HEAD_DIM), lambda h, n: (0, C_WIN // HEAD_DIM + h)),
            pl.BlockSpec((1, SEQ // Q_BLOCK, HEAD_DIM, Q_BLOCK), lambda h, n: (h, 0, 0, 0)),
            pl.BlockSpec((N_SLC_P, N_CMP), lambda h, n: (0, 0)),
            pl.BlockSpec((1, 16, Q_BLOCK), lambda h, n: (h, 0, n)),
        ],
        out_specs=pl.BlockSpec((Q_BLOCK, NQ), lambda h, n: (n, h)),
        scratch_shapes=[pltpu.VMEM((HEAD_DIM, NQ), bf16), pltpu.VMEM((HEAD_DIM, NQ), f32),
                        pltpu.VMEM((1, NQ), f32), pltpu.VMEM((1, NQ), f32)],
        compiler_params=_cparams(("arbitrary", "arbitrary")),
        name="nsa_prompt",
    )(p, kcmp, kcmp, pb, vst, pb, vwt, mt, gt)


NS_PAD = 256
WB = min(WINDOW, PAST_LEN)


def _nsa_sample_body(tk_ref, tv_ref, *refs):
    kpages = refs[:PG]
    vpages = refs[PG:2 * PG]
    (qt_ref, kc_ref, vc_ref, cw_ref, kvn_ref, wn_ref, mt_ref, gt_ref, o_ref,
     sel_ref, acc_ref, m_ref, l_ref, oc_ref, ow_ref) = refs[2 * PG:]
    g = pl.program_id(1)
    col = lax.broadcasted_iota(i32, (1, LANES), 1)
    col_h = col // (GROUP * DEC_SEQ)
    qpos = PAST_LEN + col % DEC_SEQ
    qt = qt_ref[0]

    def online(s, valid, vt):
        sm = jnp.where(valid, s, MASK_VALUE)
        m_old = m_ref[...]
        m_new = jnp.maximum(m_old, jnp.max(sm, axis=0, keepdims=True))
        alpha = jnp.exp(m_old - m_new)
        e = jnp.where(valid, jnp.exp(sm - m_new), 0.0)
        l_ref[...] = alpha * l_ref[...] + jnp.sum(e, axis=0, keepdims=True)
        acc_ref[...] = alpha * acc_ref[...] + jnp.dot(vt, e.astype(bf16), preferred_element_type=f32)
        m_ref[...] = m_new

    @pl.when(g == 0)
    def _():
        s0 = jnp.dot(kc_ref[0, 0].astype(bf16), qt, preferred_element_type=f32)
        s1 = jnp.dot(kc_ref[0, 1].astype(bf16), qt, preferred_element_type=f32)
        sc = jnp.where(col_h == 0, s0, s1) * SCALE
        cmp_end = lax.broadcasted_iota(i32, (N_CMP, 1), 0) * CMP_STRIDE + (CMP_BLOCK - 1)
        pc = _col_softmax(sc, cmp_end <= qpos)
        pcb = pc.astype(bf16)
        o0 = jnp.dot(vc_ref[0, 0].T.astype(bf16), pcb, preferred_element_type=f32)
        o1 = jnp.dot(vc_ref[0, 1].T.astype(bf16), pcb, preferred_element_type=f32)
        oc_ref[...] = jnp.where(col_h == 0, o0, o1)
        psum = pc
        for k in range(1, GROUP):
            psum = psum + pltpu.roll(pc, k * DEC_SEQ, 1)
        imp = jnp.dot(mt_ref[...], psum.astype(bf16), preferred_element_type=f32)
        jb = lax.broadcasted_iota(i32, (NS_PAD, 1), 0)
        imp = _force_and_mask(imp, jb, qpos)
        imp = jnp.where(jb < N_SLC_S, imp, NEG_BIG)
        sel = _select_blocks(imp, NS_PAD)
        last = (col // DEC_SEQ) % GROUP == GROUP - 1
        sel = jnp.where(last, sel, 0.0)
        full = sel
        for k in range(1, GROUP):
            full = full + pltpu.roll(sel, LANES - k * DEC_SEQ, 1)
        sel_ref[...] = full
        rw = lax.broadcasted_iota(i32, (WB * N_KV_HEADS, 1), 0)
        sw = jnp.dot(cw_ref[0, 0].astype(bf16), qt, preferred_element_type=f32) * SCALE
        dtw = qpos - (PAST_LEN - WB + rw // N_KV_HEADS)
        vw = (rw % N_KV_HEADS == col_h) & (dtw >= 0) & (dtw < WINDOW)
        rn = lax.broadcasted_iota(i32, (LANES, 1), 0)
        sn = jnp.dot(wn_ref[0, 0].astype(bf16), qt, preferred_element_type=f32) * SCALE
        dtn = qpos - (PAST_LEN + rn // N_KV_HEADS)
        vn = (rn < DEC_SEQ * N_KV_HEADS) & (rn % N_KV_HEADS == col_h) & (dtn >= 0) & (dtn < WINDOW)
        swm = jnp.where(vw, sw, MASK_VALUE)
        snm = jnp.where(vn, sn, MASK_VALUE)
        mx = jnp.maximum(jnp.max(swm, axis=0, keepdims=True), jnp.max(snm, axis=0, keepdims=True))
        ew = jnp.where(vw, jnp.exp(swm - mx), 0.0)
        en = jnp.where(vn, jnp.exp(snm - mx), 0.0)
        den = jnp.maximum(jnp.sum(ew, axis=0, keepdims=True) + jnp.sum(en, axis=0, keepdims=True), 1e-30)
        ow = (jnp.dot(cw_ref[0, 1].T.astype(bf16), ew.astype(bf16), preferred_element_type=f32)
              + jnp.dot(wn_ref[0, 1].T.astype(bf16), en.astype(bf16), preferred_element_type=f32))
        ow_ref[...] = ow / den
        m_ref[...] = jnp.full((1, LANES), MASK_VALUE, f32)
        l_ref[...] = jnp.zeros((1, LANES), f32)
        acc_ref[...] = jnp.zeros((HEAD_DIM, LANES), f32)

    rp = lax.broadcasted_iota(i32, (PAGE_ROWS, 1), 0)
    head_ok = rp % N_KV_HEADS == col_h
    for i in range(PG):
        pidx = g * PG + i
        s = jnp.dot(kpages[i][0].astype(bf16), qt, preferred_element_type=f32) * SCALE
        m0 = sel_ref[pl.ds(2 * pidx, 1), :]
        m1 = sel_ref[pl.ds(2 * pidx + 1, 1), :]
        mk = jnp.where(rp < PAGE_ROWS // 2, m0, m1)
        valid = head_ok & (mk > 0.5) & (pidx * PAGE_SIZE + rp // N_KV_HEADS <= qpos)
        online(s, valid, vpages[i][0].T.astype(bf16))

    @pl.when(g == N_PAGES // PG - 1)
    def _():
        rn = lax.broadcasted_iota(i32, (LANES, 1), 0)
        s = jnp.dot(kvn_ref[0, 2].astype(bf16), qt, preferred_element_type=f32) * SCALE
        mk = sel_ref[pl.ds(PAST_LEN // SLC_BLOCK, 1), :]
        valid = ((rn < DEC_SEQ * N_KV_HEADS) & (rn % N_KV_HEADS == col_h) & (mk > 0.5)
                 & (PAST_LEN + rn // N_KV_HEADS <= qpos))
        online(s, valid, kvn_ref[0, 3].T.astype(bf16))
        os_ = acc_ref[...] / jnp.maximum(l_ref[...], 1e-30)
        gts = _sigmoid(gt_ref[0])
        out = gts[0:1] * oc_ref[...] + gts[1:2] * os_ + gts[2:3] * ow_ref[...]
        o_ref[0] = out.T


def _nsa_sample(pages, tk, tv, qt, kcmp, cwin, kv_new, win_new, mt, gt):
    def page_spec(i, which):
        return pl.BlockSpec((1, PAGE_ROWS, HEAD_DIM),
                            lambda b, g, tk, tv: ((tk, tv)[which][b, g * PG + i], 0, 0))

    per_b = lambda b, g, tk, tv: (b, 0, 0, 0)
    sq = (HEAD_DIM, LANES)
    return pl.pallas_call(
        _nsa_sample_body,
        out_shape=jax.ShapeDtypeStruct((DEC_BATCH, LANES, HEAD_DIM), f32),
        grid_spec=pltpu.PrefetchScalarGridSpec(
            num_scalar_prefetch=2,
            grid=(DEC_BATCH, N_PAGES // PG),
            in_specs=[page_spec(i, 0) for i in range(PG)] + [page_spec(i, 1) for i in range(PG)] + [
                pl.BlockSpec((1, HEAD_DIM, LANES), lambda b, g, tk, tv: (b, 0, 0)),
                pl.BlockSpec((1, N_KV_HEADS, N_CMP, HEAD_DIM), lambda b, g, tk, tv: (2 * b, 0, 0, 0)),
                pl.BlockSpec((1, N_KV_HEADS, N_CMP, HEAD_DIM), lambda b, g, tk, tv: (2 * b + 1, 0, 0, 0)),
                pl.BlockSpec((1, 2, WB * N_KV_HEADS, HEAD_DIM), per_b),
                pl.BlockSpec((1, 4, LANES, HEAD_DIM), per_b),
                pl.BlockSpec((1, 2, LANES, HEAD_DIM), per_b),
                pl.BlockSpec((NS_PAD, N_CMP), lambda b, g, tk, tv: (0, 0)),
                pl.BlockSpec((1, SUBLANES, LANES), lambda b, g, tk, tv: (b, 0, 0)),
            ],
            out_specs=pl.BlockSpec((1, LANES, HEAD_DIM), lambda b, g, tk, tv: (b, 0, 0)),
            scratch_shapes=[pltpu.VMEM((NS_PAD, LANES), f32), pltpu.VMEM(sq, f32), pltpu.VMEM((1, LANES), f32),
                            pltpu.VMEM((1, LANES), f32), pltpu.VMEM(sq, f32), pltpu.VMEM(sq, f32)],
        ),
        compiler_params=_cparams(("arbitrary", "arbitrary")),
        name="nsa_sample",
    )(tk, tv, *([pages] * (2 * PG)), qt, kcmp, kcmp, cwin, kv_new, win_new, mt, gt)


HALO = 32
RC = 32
LC = 512


def _ln_swish(y, g, b):
    mu = jnp.mean(y, axis=-1, keepdims=True)
    yc = y - mu
    var = jnp.mean(yc * yc, axis=-1, keepdims=True)
    z = yc * lax.rsqrt(var + EPS) * g + b
    return z * _sigmoid(z)


def _conv_prompt_body(ua_ref, ub_ref, cw_ref, cb_ref, lg_ref, lb_ref, o_ref, last_ref, g_ref, y_ref):
    i = pl.program_id(0)

    @pl.when(i == 0)
    def _():
        g_ref[0:HALO, :] = jnp.zeros((HALO, D_CONV), f32)

    @pl.when(i > 0)
    def _():
        g_ref[0:HALO, :] = g_ref[TM:TM + HALO, :]

    glu = ua_ref[...] * _sigmoid(ub_ref[...])
    g_ref[HALO:HALO + TM, :] = glu
    last_ref[...] = glu
    off = HALO - (CONV_WIDTH - 1)

    def rows(k, c):
        r0 = pl.multiple_of(k * RC, RC)
        for lc in range(D_CONV // LC):
            ls = slice(lc * LC, (lc + 1) * LC)
            win = g_ref[pl.ds(r0, RC + HALO), ls]
            acc = jnp.zeros((RC, LC), f32)
            for w in range(CONV_WIDTH):
                acc = acc + win[off + w:off + w + RC, :] * cw_ref[w:w + 1, ls]
            y_ref[pl.ds(r0, RC), ls] = acc
        return c

    lax.fori_loop(0, TM // RC, rows, 0)
    o_ref[...] = _ln_swish(y_ref[...] + cb_ref[...], lg_ref[...], lb_ref[...]).astype(o_ref.dtype)


def _conv_prompt(p, cw, cb, lg, lb):
    vec = pl.BlockSpec((1, D_CONV), lambda i: (0, 0))
    return pl.pallas_call(
        _conv_prompt_body,
        out_shape=(jax.ShapeDtypeStruct((SEQ, D_CONV), bf16), jax.ShapeDtypeStruct((TM, D_CONV), f32)),
        grid=(SEQ // TM,),
        in_specs=[
            pl.BlockSpec((TM, D_CONV), lambda i: (i, C_UA // D_CONV)),
            pl.BlockSpec((TM, D_CONV), lambda i: (i, C_UB // D_CONV)),
            pl.BlockSpec((CONV_WIDTH, D_CONV), lambda i: (0, 0)),
            vec, vec, vec,
        ],
        out_specs=(pl.BlockSpec((TM, D_CONV), lambda i: (i, 0)), pl.BlockSpec((TM, D_CONV), lambda i: (0, 0))),
        scratch_shapes=[pltpu.VMEM((HALO + TM, D_CONV), f32), pltpu.VMEM((TM, D_CONV), f32)],
        compiler_params=_cparams(("arbitrary",)),
        name="conv_prompt",
    )(p, p, cw, cb, lg, lb)


def _conv_sample_body(prev_ref, ua_ref, ub_ref, cw_ref, cb_ref, lg_ref, lb_ref, o_ref, glu_ref):
    glu = ua_ref[...] * _sigmoid(ub_ref[...])
    glu_ref[...] = glu
    n_prev = CONV_WIDTH - 1
    for t in range(DEC_SEQ):
        acc = jnp.zeros((DEC_BATCH, D_CONV), f32)
        for w in range(CONV_WIDTH):
            k = t + w
            x = prev_ref[k] if k < n_prev else glu[k - n_prev]
            acc = acc + x * cw_ref[w:w + 1, :]
        o_ref[t] = _ln_swish(acc + cb_ref[...], lg_ref[...], lb_ref[...])


def _conv_sample(prev, ua, ub, cw, cb, lg, lb):
    return pl.pallas_call(
        _conv_sample_body,
        out_shape=(jax.ShapeDtypeStruct((DEC_SEQ, DEC_BATCH, D_CONV), f32),
                   jax.ShapeDtypeStruct((DEC_SEQ, DEC_BATCH, D_CONV), f32)),
        name="conv_sample",
    )(prev, ua, ub, cw, cb, lg, lb)


def _overlap_t(nc, ns, ns_pad):
    ci = np.arange(nc)[None, :] * CMP_STRIDE
    sj = np.arange(ns_pad)[:, None] * SLC_BLOCK
    ov = np.minimum(ci + CMP_BLOCK, sj + SLC_BLOCK) - np.maximum(ci, sj)
    ov = np.clip(ov, 0, None).astype(np.float32) / CMP_BLOCK
    ov[ns:] = 0.0
    return jnp.asarray(ov, dtype=bf16)


def _moe_plan(top_i):
    n_assign = 2 * N_VALID
    cap_tiles = (n_assign + N_EXPERTS * (TM - 1)) // TM + 1
    e_flat = top_i[:N_VALID, :2].reshape(-1)
    onehot = (e_flat[:, None] == jnp.arange(N_EXPERTS, dtype=i32)[None, :]).astype(i32)
    csum = jnp.cumsum(onehot, axis=0)
    counts = csum[-1]
    rank = jnp.take_along_axis(csum, e_flat[:, None], axis=1)[:, 0] - 1
    tiles_e = (counts + TM - 1) // TM
    tile_end = jnp.cumsum(tiles_e)
    tile_start = tile_end - tiles_e
    slot = tile_start[e_flat] * TM + rank
    n_used = tile_end[-1].astype(i32).reshape(1)
    src = jnp.zeros((cap_tiles * TM,), i32).at[slot].set(jnp.arange(n_assign, dtype=i32) // 2)
    te = jnp.sum((jnp.arange(cap_tiles, dtype=i32)[:, None] >= tile_end[None, :]).astype(i32), axis=1)
    te = jnp.minimum(te, N_EXPERTS - 1).astype(i32)
    slot2 = slot.reshape(N_VALID, 2).astype(i32)
    pad = jnp.broadcast_to(slot2[0:1], (A_ROWS - N_VALID, 2))
    slot2 = jnp.concatenate([slot2, pad], axis=0)
    return src, te, n_used, slot2[:, 0], slot2[:, 1], cap_tiles * TM


def kernel(x_prompt, x_sample, cache_kv, cache_win, cache_conv, page_table, c_prompt, c_sample, w_ada, b_ada, norm1_g, norm2_g, w_in, w_out, cmp_pe, cmp_w1, cmp_b1, cmp_w2, cmp_b2, conv_w, conv_b, conv_norm_g, conv_norm_b, ffn_w1, ffn_w3, ffn_w2, moe_router_w, moe_router_b, moe_w1, moe_w3, moe_w2, final_norm_g):
    n_pool = cache_kv.shape[0]
    pad_rows = A_ROWS - N_VALID
    x = jnp.concatenate([x_prompt[0], x_sample.reshape(N_SAMPLE, D_MODEL), jnp.zeros((pad_rows, D_MODEL), f32)], axis=0)
    c_all = jnp.concatenate([c_prompt, c_sample, jnp.zeros((7, D_MODEL), f32)], axis=0)
    mod_rows = jnp.concatenate([
        jnp.zeros((TM,), i32),
        1 + jnp.arange(N_SAMPLE, dtype=i32) // DEC_SEQ,
        jnp.zeros((TM - N_SAMPLE,), i32)]).reshape(2, TM)
    te_dense = jnp.zeros((N_TILES,), i32)
    nu_dense = jnp.full((1,), N_TILES, i32)
    mt_p = _overlap_t(N_CMP, N_SLC_P, N_SLC_P)
    mt_s = _overlap_t(N_CMP, N_SLC_S, NS_PAD)
    pages_s = cache_kv.reshape(n_pool * DEPTH * 4, PAGE_ROWS, HEAD_DIM)
    seq_pages = jnp.arange(N_PAGES, dtype=i32)

    kv_p, win_p, conv_p, kv_s, win_s, conv_s = [], [], [], [], [], []
    for l in range(DEPTH):
        ada = _adaln(c_all, w_ada[l], b_ada[l][None, :])
        mod = ada[mod_rows]
        h1 = _modulate(x, norm1_g[l][None, :], mod, 0, 1)

        wl = w_in[l]
        c0 = D_ATTN + 6 * D_KV
        w_main = jnp.concatenate([wl[:, :D_ATTN], wl[:, c0 + N_GATE:], wl[:, D_ATTN:c0]], axis=1)[None]
        w_gate = jnp.pad(wl[:, c0:c0 + N_GATE], ((0, 0), (0, LANES - N_GATE)))[None]
        p, pb = _matmul([h1], w_main, te_dense, nu_dense, tn=512, out_dtypes=(f32, bf16), name="proj")
        (gl,) = _matmul([h1], w_gate, te_dense, nu_dense, tn=LANES, out_dtypes=(f32,), name="proj_gate")

        kvp = p[:SEQ, C_KV:C_KV + 4 * D_KV].reshape(SEQ, 4, N_KV_HEADS, HEAD_DIM).transpose(1, 0, 2, 3)
        winp = p[:SEQ, C_WIN:C_WIN + 2 * D_KV].reshape(SEQ, 2, N_KV_HEADS, HEAD_DIM).transpose(1, 0, 2, 3)
        ps = p[SEQ:N_VALID]
        kvs = ps[:, C_KV:C_KV + 4 * D_KV].reshape(DEC_BATCH, DEC_SEQ, 4, N_KV_HEADS, HEAD_DIM).transpose(0, 2, 1, 3, 4)
        wins = ps[:, C_WIN:C_WIN + 2 * D_KV].reshape(DEC_BATCH, DEC_SEQ, 2, N_KV_HEADS, HEAD_DIM).transpose(0, 2, 1, 3, 4)
        kv_p.append(kvp[None])
        win_p.append(winp[None, :, SEQ - min(WINDOW, SEQ):])
        kv_s.append(kvs)
        wfull = jnp.concatenate([cache_win[l], wins], axis=2)
        win_s.append(wfull[:, :, -WB:])

        cmp_w = (cmp_w1[l], cmp_b1[l][:, None, :], cmp_pe[l].reshape(2, 1, CMP_BLOCK * HEAD_DIM),
                 cmp_w2[l], cmp_b2[l][:, None, :])
        pages_p = kvp.reshape(4 * N_PAGES, PAGE_ROWS, HEAD_DIM)
        tab_p = jnp.stack([seq_pages, N_PAGES + seq_pages], axis=0)
        kcmp_p = _compress(pages_p, tab_p, jnp.zeros((2, SUBLANES, HEAD_DIM), f32), *cmp_w)
        tab_s = (page_table[:, None, :] * DEPTH + l) * 4 + jnp.arange(2, dtype=i32)[None, :, None]
        new_s = kvs[:, 0:2].reshape(2 * DEC_BATCH, DEC_SEQ * N_KV_HEADS, HEAD_DIM)
        kcmp_s = _compress(pages_s, tab_s.reshape(2 * DEC_BATCH, N_PAGES).astype(i32), new_s, *cmp_w)

        vs_b = pb[:SEQ, C_KV + 3 * D_KV:C_KV + 4 * D_KV].reshape(SEQ // KT, KT, N_KV_HEADS, HEAD_DIM)
        vst = vs_b.transpose(2, 0, 3, 1)
        vw_b = pb[:SEQ, C_WIN + D_KV:C_WIN + 2 * D_KV].reshape(SEQ // Q_BLOCK, Q_BLOCK, N_KV_HEADS, HEAD_DIM)
        vwt = vw_b.transpose(2, 0, 3, 1)
        gtp = gl[:SEQ, :N_GATE].reshape(SEQ, N_KV_HEADS, GROUP * 3).transpose(1, 2, 0)
        gtp = jnp.pad(gtp, ((0, 0), (0, 16 - GROUP * 3), (0, 0)))
        attn_p = _nsa_prompt(p, pb, kcmp_p, vst, vwt, mt_p, gtp)

        tk = (page_table * DEPTH + l) * 4 + 2
        tv = tk + 1
        q_s = ps[:, :D_ATTN].reshape(DEC_BATCH, DEC_SEQ, N_HEADS, HEAD_DIM).transpose(0, 3, 2, 1)
        qt_s = jnp.pad(q_s.reshape(DEC_BATCH, HEAD_DIM, N_HEADS * DEC_SEQ),
                       ((0, 0), (0, 0), (0, LANES - N_HEADS * DEC_SEQ))).astype(bf16)
        rows_pad = LANES - DEC_SEQ * N_KV_HEADS
        kv_new = jnp.pad(kvs.reshape(DEC_BATCH, 4, DEC_SEQ * N_KV_HEADS, HEAD_DIM), ((0, 0), (0, 0), (0, rows_pad), (0, 0)))
        win_new = jnp.pad(wins.reshape(DEC_BATCH, 2, DEC_SEQ * N_KV_HEADS, HEAD_DIM), ((0, 0), (0, 0), (0, rows_pad), (0, 0)))
        cwin = cache_win[l].reshape(DEC_BATCH, 2, WB * N_KV_HEADS, HEAD_DIM)
        gts = gl[SEQ:N_VALID, :N_GATE].reshape(DEC_BATCH, DEC_SEQ, N_HEADS, 3).transpose(0, 3, 2, 1)
        gts = jnp.pad(gts.reshape(DEC_BATCH, 3, N_HEADS * DEC_SEQ), ((0, 0), (0, SUBLANES - 3), (0, LANES - N_HEADS * DEC_SEQ)))
        o_s = _nsa_sample(pages_s, tk, tv, qt_s, kcmp_s, cwin, kv_new, win_new, mt_s, gts)
        attn_s = o_s[:, :N_HEADS * DEC_SEQ].reshape(DEC_BATCH, N_HEADS, DEC_SEQ, HEAD_DIM)
        attn_s = attn_s.transpose(0, 2, 1, 3).reshape(N_SAMPLE, D_ATTN).astype(bf16)

        cvec = (conv_b[l][None, :], conv_norm_g[l][None, :], conv_norm_b[l][None, :])
        conv_pr, glu_last = _conv_prompt(p, conv_w[l], *cvec)
        conv_p.append(glu_last[None, TM - (CONV_WIDTH - 1):])
        ua_s = ps[:, C_UA:C_UA + D_CONV].reshape(DEC_BATCH, DEC_SEQ, D_CONV).transpose(1, 0, 2)
        ub_s = ps[:, C_UB:C_UB + D_CONV].reshape(DEC_BATCH, DEC_SEQ, D_CONV).transpose(1, 0, 2)
        conv_sm, glu_s = _conv_sample(cache_conv[l].transpose(1, 0, 2), ua_s, ub_s, conv_w[l], *cvec)
        xcat_s = jnp.concatenate([cache_conv[l], glu_s.transpose(1, 0, 2)], axis=1)
        conv_s.append(xcat_s[:, -(CONV_WIDTH - 1):])

        pad_b = jnp.zeros((pad_rows, D_ATTN), bf16)
        attn = jnp.concatenate([attn_p, attn_s, pad_b], axis=0)
        conv = jnp.concatenate([conv_pr, conv_sm.transpose(1, 0, 2).reshape(N_SAMPLE, D_CONV).astype(bf16), pad_b], axis=0)
        (x,) = _matmul([attn, conv], w_out[l][None], te_dense, nu_dense, tn=512, out_dtypes=(f32,),
                       res=x, gate=mod, gate_col=2, name="merge_out")

        i = l // 2
        if l % 2 == 0:
            h2 = _modulate(x, norm2_g[l][None, :], mod, 3, 4)
            hid = _swiglu_up(h2, ffn_w1[i][None], ffn_w3[i][None], te_dense, nu_dense)
            (x,) = _matmul([hid], ffn_w2[i][None], te_dense, nu_dense, tn=512, out_dtypes=(f32,),
                           res=x, gate=mod, gate_col=5, vmem_mb=56, name="ffn_down")
        else:
            rw = jnp.pad(moe_router_w[i], ((0, 0), (0, LANES - N_EXPERTS)))
            rb = jnp.pad(moe_router_b[i], (0, LANES - N_EXPERTS))[None, :]
            h2, _, top_i, top_p = _modulate(x, norm2_g[l][None, :], mod, 3, 4, router=(rw, rb))
            src, te, nu, slot0, slot1, n_sorted = _moe_plan(top_i)
            xs = jnp.take(h2, src, axis=0)
            hid = _swiglu_up(xs, moe_w1[i], moe_w3[i], te, nu)
            (y,) = _matmul([hid], moe_w2[i], te, nu, tn=512, out_dtypes=(f32,), vmem_mb=56, name="moe_down")
            x = _combine(jnp.take(y, slot0, axis=0), jnp.take(y, slot1, axis=0), x, mod, 5, top_p)

    y_all = _final_norm(x, final_norm_g[None, :])
    y_prompt = y_all[:SEQ][None]
    y_sample = y_all[SEQ:N_VALID].reshape(DEC_BATCH, DEC_SEQ, D_MODEL)
    return (y_prompt, y_sample,
            jnp.stack(kv_p, axis=1), jnp.stack(kv_s, axis=1),
            jnp.stack(win_p, axis=0), jnp.stack(win_s, axis=0),
            jnp.stack(conv_p, axis=0), jnp.stack(conv_s, axis=0))
```
